```python
import math, functools
import jax, jax.numpy as jnp
from jax import lax
import numpy as np

D_MODEL = 1024
BATCH = 8
SEQ = 2048
DEPTH = 1
DEC_BATCH = 32
DEC_SEQ = 64
PAST_LEN = 4096

CHUNK = 64
Q_BLOCK = 128
FOX_HEADS = D_MODEL // 128
FOX_DH = 64
DIFF_HEADS = D_MODEL // 256
DIFF_DH = 64
FOX_W = FOX_HEADS * FOX_DH
DIFF_W = DIFF_HEADS * 2 * DIFF_DH
MIX_W = FOX_W + DIFF_W
IN_W = 3 * FOX_W + FOX_HEADS + 3 * DIFF_W
REL_BUCKETS = 32
REL_MAX_DIST = 128
PEER_HEADS = 8
PEER_KEYS = 128
PEER_EXPERTS = PEER_KEYS * PEER_KEYS
PEER_QDIM = 256
PEER_TOPK = 16
PEER_BLOCK = 128
LN_EPS = 1e-5
NEG_INF = -1e30
DEEP_ALPHA = (2 * DEPTH) ** 0.25
DEEP_BETA = (8 * DEPTH) ** -0.25

kernel_name = "hybrid_fox_diffattn_peer_stream_step"


def _layer_norm(x, g, b):
    xf = x.astype(jnp.float32)
    mu = jnp.mean(xf, -1, keepdims=True)
    var = jnp.mean(jnp.square(xf - mu), -1, keepdims=True)
    return ((xf - mu) * lax.rsqrt(var + LN_EPS) * g.astype(jnp.float32) + b.astype(jnp.float32)).astype(x.dtype)


def _rms_norm(x, g):
    xf = x.astype(jnp.float32)
    return (xf * lax.rsqrt(jnp.mean(jnp.square(xf), -1, keepdims=True) + LN_EPS) * g.astype(jnp.float32)).astype(x.dtype)


def _rel_bucket(rel):
    half = REL_BUCKETS // 2
    max_exact = half // 2
    ret = jnp.where(rel > 0, half, 0)
    n = jnp.abs(rel)
    nf = jnp.maximum(n, 1).astype(jnp.float32)
    large = max_exact + (jnp.log(nf / max_exact) / math.log(REL_MAX_DIST / max_exact) * (half - max_exact)).astype(jnp.int32)
    large = jnp.minimum(large, half - 1)
    return ret + jnp.where(n < max_exact, n, large)


def _sweep(block_fn, n_q):
    qblk = min(Q_BLOCK, n_q)
    nblk = n_q // qblk
    out = lax.map(lambda i: block_fn(i * qblk, qblk), jnp.arange(nblk))
    out = jnp.moveaxis(out, 0, 1)
    return out.reshape(out.shape[0], n_q, *out.shape[3:])


def _fox_attend(q, k, v, d_q, d_k, q_off):
    pos_k = jnp.arange(k.shape[1])
    d_qT = jnp.swapaxes(d_q, 1, 2)
    d_kT = jnp.swapaxes(d_k, 1, 2)
    scale = FOX_DH ** -0.5

    def block(s, qblk):
        qb = lax.dynamic_slice_in_dim(q, s, qblk, 1)
        dqb = lax.dynamic_slice_in_dim(d_qT, s, qblk, 2)
        pos_q = q_off + s + jnp.arange(qblk)
        logits = jnp.einsum('bqhd,bkhd->bhqk', qb, k, preferred_element_type=jnp.float32) * scale
        logits = logits + (dqb[..., :, None] - d_kT[..., None, :])
        mask = pos_k[None, :] <= pos_q[:, None]
        p = jax.nn.softmax(jnp.where(mask, logits, NEG_INF), axis=-1)
        return jnp.einsum('bhqk,bkhd->bqhd', p.astype(v.dtype), v)

    return _sweep(block, q.shape[1])


def _diff_attend(q1, q2, k1, k2, v, lam, rel_table, q_off):
    pos_k = jnp.arange(k1.shape[1])
    scale = DIFF_DH ** -0.5

    def block(s, qblk):
        pos_q = q_off + s + jnp.arange(qblk)
        bias = jnp.moveaxis(rel_table[_rel_bucket(pos_k[None, :] - pos_q[:, None])], -1, 0).astype(jnp.float32)
        mask = (pos_k[None, :] // CHUNK) <= (pos_q[:, None] // CHUNK)

        def probs(qx, kx):
            qb = lax.dynamic_slice_in_dim(qx, s, qblk, 1)
            lg = jnp.einsum('bqhd,bkhd->bhqk', qb, kx, preferred_element_type=jnp.float32) * scale + bias
            return jax.nn.softmax(jnp.where(mask, lg, NEG_INF), axis=-1)

        w = probs(q1, k1) - lam * probs(q2, k2)
        return jnp.einsum('bhqk,bkhe->bqhe', w.astype(v.dtype), v)

    return _sweep(block, q1.shape[1])


def _token_mixers(x, past, layer_idx, w_in, b_f, lam_q1, lam_k1, lam_q2, lam_k2, subln_g, w_out, rel_table):
    B, T, _ = x.shape
    h = x @ w_in
    cuts = [FOX_W, 2 * FOX_W, 3 * FOX_W, 3 * FOX_W + FOX_HEADS,
            3 * FOX_W + FOX_HEADS + DIFF_W, 3 * FOX_W + FOX_HEADS + 2 * DIFF_W]
    fq, fk, fv, fg, dq, dk, dv = jnp.split(h, cuts, axis=-1)
    fq = fq.reshape(B, T, FOX_HEADS, FOX_DH)
    fk = fk.reshape(B, T, FOX_HEADS, FOX_DH)
    fv = fv.reshape(B, T, FOX_HEADS, FOX_DH)
    logf = jax.nn.log_sigmoid(fg.astype(jnp.float32) + b_f.astype(jnp.float32))
    dq = dq.reshape(B, T, DIFF_HEADS, 2, DIFF_DH)
    dk = dk.reshape(B, T, DIFF_HEADS, 2 * DIFF_DH)
    dv = dv.reshape(B, T, DIFF_HEADS, 2 * DIFF_DH)

    if past is None:
        fk_all, fv_all, logf_all, dk_all, dv_all = fk, fv, logf, dk, dv
    else:
        c_fk, c_fv, c_lf, c_dk, c_dv = past
        fk_all = jnp.concatenate([c_fk, fk], axis=1)
        fv_all = jnp.concatenate([c_fv, fv], axis=1)
        logf_all = jnp.concatenate([c_lf.astype(jnp.float32), logf], axis=1)
        dk_all = jnp.concatenate([c_dk, dk], axis=1)
        dv_all = jnp.concatenate([c_dv, dv], axis=1)
    P = fk_all.shape[1] - T

    d_cum = jnp.cumsum(logf_all, axis=1)
    fox = _fox_attend(fq, fk_all, fv_all, d_cum[:, P:], d_cum, P)

    lam_init = 0.8 - 0.6 * math.exp(-0.3 * layer_idx)
    lam = (jnp.exp(jnp.sum(lam_q1.astype(jnp.float32) * lam_k1.astype(jnp.float32)))
           - jnp.exp(jnp.sum(lam_q2.astype(jnp.float32) * lam_k2.astype(jnp.float32))) + lam_init)
    dk_all = dk_all.reshape(dk_all.shape[0], dk_all.shape[1], DIFF_HEADS, 2, DIFF_DH)
    diff = _diff_attend(dq[..., 0, :], dq[..., 1, :], dk_all[..., 0, :], dk_all[..., 1, :], dv_all, lam, rel_table, P)
    diff = _rms_norm(diff, subln_g) * (1.0 - lam_init)

    mixed = jnp.concatenate([fox.reshape(B, T, FOX_W), diff.reshape(B, T, DIFF_W)], axis=-1)
    return mixed @ w_out, (fk, fv, logf, dk, dv)


def _peer(x, w_pq, sub_keys, u_tab, v_tab):
    B, T, D = x.shape
    n = B * T
    pad = (-n) % PEER_BLOCK
    xb = jnp.pad(x.reshape(n, D), ((0, pad), (0, 0))).reshape(-1, PEER_BLOCK, D)

    def block(xs):
        q = (xs @ w_pq).reshape(PEER_BLOCK, PEER_HEADS, 2, PEER_QDIM // 2)
        sc = jnp.einsum('thpd,hpnd->thpn', q, sub_keys, preferred_element_type=jnp.float32)
        s, i = lax.top_k(sc, PEER_TOPK)
        cand = (s[:, :, 0, :, None] + s[:, :, 1, None, :]).reshape(PEER_BLOCK, PEER_HEADS, PEER_TOPK * PEER_TOPK)
        cidx = (i[:, :, 0, :, None] * PEER_KEYS + i[:, :, 1, None, :]).reshape(PEER_BLOCK, PEER_HEADS, PEER_TOPK * PEER_TOPK)
        top, sel = lax.top_k(cand, PEER_TOPK)
        eidx = jnp.take_along_axis(cidx, sel, axis=-1)
        g = jax.nn.softmax(top, axis=-1)
        hid = jnp.einsum('td,thkd->thk', xs, u_tab[eidx])
        a = (g * jax.nn.gelu(hid.astype(jnp.float32), approximate=False)).astype(xs.dtype)
        return jnp.einsum('thk,thkd->td', a, v_tab[eidx])

    y = lax.map(block, xb).reshape(-1, D)[:n]
    return y.reshape(B, T, D)


def _trunk(x, past, w_in, b_f, lam_q1, lam_k1, lam_q2, lam_k2, subln_g, w_out, rel_bias,
           ln1_g, ln1_b, ln2_g, ln2_b, peer_wq, peer_subkeys, peer_u, peer_v):
    rows = []
    for l in range(DEPTH):
        past_l = None if past is None else tuple(c[l] for c in past)
        attn, new = _token_mixers(x, past_l, l, w_in[l], b_f[l], lam_q1[l], lam_k1[l], lam_q2[l], lam_k2[l],
                                  subln_g[l], w_out[l], rel_bias)
        x = _layer_norm(DEEP_ALPHA * x + attn, ln1_g[l], ln1_b[l])
        x = _layer_norm(DEEP_ALPHA * x + _peer(x, peer_wq[l], peer_subkeys[l], peer_u[l], peer_v[l]), ln2_g[l], ln2_b[l])
        rows.append(new)
    stacked = [jnp.stack(r, axis=0) for r in zip(*rows)]
    return x, stacked


def setup_inputs(seed: int = 0) -> dict:
    key = jax.random.key(seed)
    ks = jax.random.split(key, 32)
    f32 = jnp.float32
    nrm = lambda k, shape, s=1.0: jax.random.normal(k, shape, f32) * s
    col_scale = jnp.concatenate([
        jnp.ones((2 * FOX_W,), f32), jnp.full((FOX_W,), DEEP_BETA, f32),
        jnp.ones((FOX_HEADS + 2 * DIFF_W,), f32), jnp.full((DIFF_W,), DEEP_BETA, f32)])
    return {
        "x_prompt": nrm(ks[0], (BATCH, SEQ, D_MODEL)),
        "x_sample": nrm(ks[1], (DEC_BATCH, DEC_SEQ, D_MODEL)),
        "cache_fox_k": nrm(ks[2], (DEPTH, DEC_BATCH, PAST_LEN, FOX_HEADS, FOX_DH)),
        "cache_fox_v": nrm(ks[3], (DEPTH, DEC_BATCH, PAST_LEN, FOX_HEADS, FOX_DH), DEEP_BETA),
        "cache_fox_logf": jax.nn.log_sigmoid(jax.random.uniform(ks[4], (DEPTH, DEC_BATCH, PAST_LEN, FOX_HEADS), f32, 1.0, 5.0)),
        "cache_diff_k": nrm(ks[5], (DEPTH, DEC_BATCH, PAST_LEN, DIFF_HEADS, 2 * DIFF_DH)),
        "cache_diff_v": nrm(ks[6], (DEPTH, DEC_BATCH, PAST_LEN, DIFF_HEADS, 2 * DIFF_DH), DEEP_BETA),
        "w_in": nrm(ks[7], (DEPTH, D_MODEL, IN_W), D_MODEL ** -0.5) * col_scale,
        "b_f": jax.random.uniform(ks[8], (DEPTH, FOX_HEADS), f32, 1.0, 4.0),
        "lam_q1": nrm(ks[9], (DEPTH, DIFF_DH), 0.1),
        "lam_k1": nrm(ks[10], (DEPTH, DIFF_DH), 0.1),
        "lam_q2": nrm(ks[11], (DEPTH, DIFF_DH), 0.1),
        "lam_k2": nrm(ks[12], (DEPTH, DIFF_DH), 0.1),
        "subln_g": 1.0 + nrm(ks[13], (DEPTH, 2 * DIFF_DH), 0.02),
        "w_out": nrm(ks[14], (DEPTH, MIX_W, D_MODEL), MIX_W ** -0.5 * DEEP_BETA),
        "rel_bias": nrm(ks[15], (REL_BUCKETS, DIFF_HEADS), 0.5),
        "ln1_g": 1.0 + nrm(ks[16], (DEPTH, D_MODEL), 0.02),
        "ln1_b": nrm(ks[17], (DEPTH, D_MODEL), 0.02),
        "ln2_g": 1.0 + nrm(ks[18], (DEPTH, D_MODEL), 0.02),
        "ln2_b": nrm(ks[19], (DEPTH, D_MODEL), 0.02),
        "peer_wq": nrm(ks[20], (DEPTH, D_MODEL, PEER_HEADS * PEER_QDIM), D_MODEL ** -0.5),
        "peer_subkeys": nrm(ks[21], (DEPTH, PEER_HEADS, 2, PEER_KEYS, PEER_QDIM // 2), (PEER_QDIM // 2) ** -0.5),
        "peer_u": nrm(ks[22], (DEPTH, PEER_EXPERTS, D_MODEL), D_MODEL ** -0.5),
        "peer_v": nrm(ks[23], (DEPTH, PEER_EXPERTS, D_MODEL), DEEP_BETA),
    }


def reference(x_prompt, x_sample, cache_fox_k, cache_fox_v, cache_fox_logf, cache_diff_k, cache_diff_v,
              w_in, b_f, lam_q1, lam_k1, lam_q2, lam_k2, subln_g, w_out, rel_bias,
              ln1_g, ln1_b, ln2_g, ln2_b, peer_wq, peer_subkeys, peer_u, peer_v):
    y_prompt, (p_fk, p_fv, p_lf, p_dk, p_dv) = _trunk(
        x_prompt, None, w_in, b_f, lam_q1, lam_k1, lam_q2, lam_k2, subln_g, w_out, rel_bias,
        ln1_g, ln1_b, ln2_g, ln2_b, peer_wq, peer_subkeys, peer_u, peer_v)
    y_sample, (s_fk, s_fv, s_lf, s_dk, s_dv) = _trunk(
        x_sample, (cache_fox_k, cache_fox_v, cache_fox_logf, cache_diff_k, cache_diff_v),
        w_in, b_f, lam_q1, lam_k1, lam_q2, lam_k2, subln_g, w_out, rel_bias,
        ln1_g, ln1_b, ln2_g, ln2_b, peer_wq, peer_subkeys, peer_u, peer_v)
    return (y_prompt, y_sample, p_fk, p_fv, p_lf, p_dk, p_dv, s_fk, s_fv, s_lf, s_dk, s_dv)
```

```python
import functools
import math

import numpy as np
import jax
import jax.numpy as jnp
from jax import lax
from jax.experimental import pallas as pl
from jax.experimental.pallas import tpu as pltpu

F32 = jnp.float32
BF16 = jnp.bfloat16

D_MODEL = 1024
CHUNK = 64
FOX_HEADS = 8
FOX_DH = 64
DIFF_HEADS = 4
DIFF_DH = 64
FOX_W = FOX_HEADS * FOX_DH
DIFF_W = DIFF_HEADS * 2 * DIFF_DH
REL_BUCKETS = 32
REL_MAX_DIST = 128
PEER_HEADS = 8
PEER_KEYS = 128
PEER_QDIM = 256
PEER_TOPK = 16
LN_EPS = 1e-5
NEG_INF = -1e30
DEPTH = 1
DEEP_ALPHA = (2 * DEPTH) ** 0.25
LAM_INIT = 0.8 - 0.6 * math.exp(-0.3 * 0)

LANES = 128
SUBLANES = 8
VMEM_LIMIT = 56 * 1024 * 1024


def _cparams(sem):
    return pltpu.CompilerParams(dimension_semantics=sem, vmem_limit_bytes=VMEM_LIMIT)


def _log_sigmoid(z):
    return jnp.minimum(z, 0.0) - jnp.log1p(jnp.exp(-jnp.abs(z)))


def _in_proj_kernel(x_ref, w_ref, wg_ref, wgt_ref, bf_ref, bft_ref,
                    fk_ref, fv_ref, dk_ref, dv_ref, logf_ref, logft_ref,
                    fqb_ref, fkb_ref, fvb_ref, dqb_ref, dkb_ref, dvb_ref):
    xb = x_ref[...].astype(BF16)
    outs = ((None, fqb_ref), (fk_ref, fkb_ref), (fv_ref, fvb_ref),
            (None, dqb_ref), (dk_ref, dkb_ref), (dv_ref, dvb_ref))
    for c, (o32, o16) in enumerate(outs):
        h = jnp.dot(xb, w_ref[:, c * FOX_W:(c + 1) * FOX_W], preferred_element_type=F32)
        if o32 is not None:
            o32[...] = h
        o16[...] = h.astype(BF16)
    fg = jnp.dot(xb, wg_ref[...], preferred_element_type=F32)[:, :FOX_HEADS]
    logf_ref[...] = _log_sigmoid(fg + bf_ref[...])
    fgt = lax.dot_general(wgt_ref[...], xb, (((1,), (1,)), ((), ())), preferred_element_type=F32)
    logft_ref[...] = _log_sigmoid(fgt[:FOX_HEADS] + bft_ref[...])


def _in_proj(x2d, w_main, w_g, w_gt, b_f, tm):
    n = x2d.shape[0]
    row = lambda w: pl.BlockSpec((tm, w), lambda i: (i, 0))
    full = lambda a: pl.BlockSpec(a.shape, lambda i: (0,) * a.ndim)
    bf = b_f.reshape(1, FOX_HEADS)
    bft = b_f.reshape(FOX_HEADS, 1)
    f32o = jax.ShapeDtypeStruct((n, FOX_W), F32)
    b16o = jax.ShapeDtypeStruct((n, FOX_W), BF16)
    return pl.pallas_call(
        _in_proj_kernel,
        grid=(n // tm,),
        in_specs=[row(D_MODEL), full(w_main), full(w_g), full(w_gt), full(bf), full(bft)],
        out_specs=[row(FOX_W)] * 4 + [row(FOX_HEADS), pl.BlockSpec((FOX_HEADS, tm), lambda i: (0, i))] + [row(FOX_W)] * 6,
        out_shape=[f32o] * 4 + [jax.ShapeDtypeStruct((n, FOX_HEADS), F32),
                                jax.ShapeDtypeStruct((FOX_HEADS, n), F32)] + [b16o] * 6,
        compiler_params=_cparams(("arbitrary",)),
        name="in_proj",
    )(x2d, w_main, w_g, w_gt, bf, bft)


def _cumsum_lanes(x):
    n = x.shape[-1]
    idx = lax.broadcasted_iota(jnp.int32, x.shape, x.ndim - 1)
    s = 1
    while s < n:
        x = x + jnp.where(idx >= s, pltpu.roll(x, s, x.ndim - 1), 0.0)
        s *= 2
    return x


def _half_select(x, upper):
    lane = lax.broadcasted_iota(jnp.int32, x.shape, x.ndim - 1)
    keep = (lane >= FOX_DH) if upper else (lane < FOX_DH)
    return jnp.where(keep, x, jnp.zeros_like(x))


def _qk(q, k):
    return lax.dot_general(q, k, (((1,), (1,)), ((), ())), preferred_element_type=F32)


def _fox_prompt_kernel(q_ref, k_ref, v_ref, lft_ref, o_ref, *, tq, t):
    qi = pl.program_id(1)
    t0 = qi * tq
    dcum = _cumsum_lanes(lft_ref[...])
    lane = lax.broadcasted_iota(jnp.int32, dcum.shape, 1)
    base = jnp.sum(jnp.where(lane == t0, dcum, 0.0), axis=1, keepdims=True)
    bias = base - dcum
    row = lax.broadcasted_iota(jnp.int32, (tq, t), 0) + t0
    col = lax.broadcasted_iota(jnp.int32, (tq, t), 1)
    visible = col <= row
    for g in range(FOX_HEADS // 2):
        sl = slice(g * LANES, (g + 1) * LANES)
        q2, k2, v2 = q_ref[:, sl], k_ref[:, sl], v_ref[:, sl]
        outs = []
        for j in range(2):
            h = 2 * g + j
            s = _qk(_half_select(q2, j == 1), k2) + bias[h:h + 1, :]
            s = jnp.where(visible, s, NEG_INF)
            m = jnp.max(s, axis=1, keepdims=True)
            p = jnp.exp(s - m)
            l = jnp.sum(p, axis=1, keepdims=True)
            o = jnp.dot(p.astype(BF16), v2, preferred_element_type=F32)
            outs.append(o / l)
        lane_o = lax.broadcasted_iota(jnp.int32, outs[0].shape, 1)
        o_ref[:, sl] = jnp.where(lane_o < FOX_DH, outs[0], outs[1]).astype(BF16)


def _fox_prompt(q, k, v, lft, tq):
    b, t, _ = q.shape
    kern = functools.partial(_fox_prompt_kernel, tq=tq, t=t)
    return pl.pallas_call(
        kern,
        grid=(b, t // tq),
        in_specs=[pl.BlockSpec((None, tq, FOX_W), lambda i, j: (i, j, 0)),
                  pl.BlockSpec((None, t, FOX_W), lambda i, j: (i, 0, 0)),
                  pl.BlockSpec((None, t, FOX_W), lambda i, j: (i, 0, 0)),
                  pl.BlockSpec((None, FOX_HEADS, t), lambda i, j: (i, 0, 0))],
        out_specs=pl.BlockSpec((None, tq, FOX_W), lambda i, j: (i, j, 0)),
        out_shape=jax.ShapeDtypeStruct((b, t, FOX_W), BF16),
        compiler_params=_cparams(("arbitrary", "arbitrary")),
        name="fox_prompt",
    )(q, k, v, lft)


def _fox_sample_kernel(q_ref, kn_ref, vn_ref, kc_ref, vc_ref, lfc_ref, lfn_ref, o_ref, dc_scr, dn_scr, *, tq, p):
    g = pl.program_id(1)
    dc = _cumsum_lanes(lfc_ref[...])
    dc_scr[...] = dc[:, p - 1:p] - dc
    dn_scr[...] = -_cumsum_lanes(lfn_ref[...])
    q2 = q_ref[...]
    kc = kc_ref[...].astype(BF16)
    vc = vc_ref[...].astype(BF16)
    kn, vn = kn_ref[...], vn_ref[...]
    row = lax.broadcasted_iota(jnp.int32, (tq, tq), 0)
    col = lax.broadcasted_iota(jnp.int32, (tq, tq), 1)
    outs = []
    for j in range(2):
        h = 2 * g + j
        qh = _half_select(q2, j == 1)
        sc = _qk(qh, kc) + dc_scr[pl.ds(h, 1), :]
        sn = _qk(qh, kn) + dn_scr[pl.ds(h, 1), :][:, :tq]
        sn = jnp.where(col <= row, sn, NEG_INF)
        m = jnp.maximum(jnp.max(sc, axis=1, keepdims=True), jnp.max(sn, axis=1, keepdims=True))
        pc = jnp.exp(sc - m)
        pn = jnp.exp(sn - m)
        l = jnp.sum(pc, axis=1, keepdims=True) + jnp.sum(pn, axis=1, keepdims=True)
        o = (jnp.dot(pc.astype(BF16), vc, preferred_element_type=F32)
             + jnp.dot(pn.astype(BF16), vn, preferred_element_type=F32))
        outs.append(o / l)
    lane_o = lax.broadcasted_iota(jnp.int32, outs[0].shape, 1)
    o_ref[...] = jnp.where(lane_o < FOX_DH, outs[0], outs[1]).astype(BF16)


def _fox_sample(q, kn, vn, kc, vc, lfc, lfn):
    b, tq, _ = q.shape
    p = kc.shape[1]
    kern = functools.partial(_fox_sample_kernel, tq=tq, p=p)
    new = pl.BlockSpec((None, tq, LANES), lambda i, g: (i, 0, g))
    old = pl.BlockSpec((None, p, LANES), lambda i, g: (i, 0, g))
    return pl.pallas_call(
        kern,
        grid=(b, FOX_HEADS // 2),
        in_specs=[new, new, new, old, old,
                  pl.BlockSpec((None, FOX_HEADS, p), lambda i, g: (i, 0, 0)),
                  pl.BlockSpec((None, FOX_HEADS, LANES), lambda i, g: (i, 0, 0))],
        out_specs=new,
        out_shape=jax.ShapeDtypeStruct((b, tq, FOX_W), BF16),
        scratch_shapes=[pltpu.VMEM((FOX_HEADS, p), F32), pltpu.VMEM((FOX_HEADS, LANES), F32)],
        compiler_params=_cparams(("arbitrary", "arbitrary")),
        name="fox_sample",
    )(q, kn, vn, kc, vc, lfc, lfn)


def _rel_bucket_np(rel):
    half = REL_BUCKETS // 2
    max_exact = half // 2
    ret = np.where(rel > 0, half, 0)
    n = np.abs(rel)
    nf = np.maximum(n, 1).astype(np.float32)
    ratio = np.log(nf / np.float32(max_exact)) / np.float32(math.log(REL_MAX_DIST / max_exact))
    large = max_exact + (ratio.astype(np.float32) * np.float32(half - max_exact)).astype(np.int32)
    large = np.minimum(large, half - 1)
    return (ret + np.where(n < max_exact, n, large)).astype(np.int32)


def _bias_table_kernel(bucket_ref, tbl_ref, o_ref):
    bucket = bucket_ref[...]
    for h in range(DIFF_HEADS):
        acc = jnp.zeros(bucket.shape, F32)
        for kb in range(REL_BUCKETS):
            acc = jnp.where(bucket == kb, tbl_ref[kb, h], acc)
        o_ref[h:h + 1, :] = acc


def _bias_table(rel_bias, off, length):
    bucket = jnp.asarray(_rel_bucket_np(np.arange(length, dtype=np.int64) - off).reshape(1, length))
    return pl.pallas_call(
        _bias_table_kernel,
        in_specs=[pl.BlockSpec(memory_space=pltpu.VMEM), pl.BlockSpec(memory_space=pltpu.SMEM)],
        out_specs=pl.BlockSpec(memory_space=pltpu.VMEM),
        out_shape=jax.ShapeDtypeStruct((DIFF_HEADS, length), F32),
        name="bias_table",
    )(bucket, rel_bias)


def _toeplitz(vec, rows, pad, width):
    b = jnp.broadcast_to(vec, (rows, vec.shape[1]))
    return pltpu.roll(b, 0, 1, stride=1, stride_axis=0)[:, pad:pad + width]


def _lam(lq1, lk1, lq2, lk2):
    return (jnp.exp(jnp.sum(lq1 * lk1, axis=1, keepdims=True))
            - jnp.exp(jnp.sum(lq2 * lk2, axis=1, keepdims=True)) + LAM_INIT)


def _sub_ln(o, g):
    ms = jnp.mean(o * o, axis=1, keepdims=True)
    return o * lax.rsqrt(ms + LN_EPS) * g * (1.0 - LAM_INIT)


def _softmax_parts(s):
    m = jnp.max(s, axis=1, keepdims=True)
    e = jnp.exp(s - m)
    return e, jnp.sum(e, axis=1, keepdims=True)


def _diff_prompt_kernel(q_ref, k_ref, v_ref, f_ref, lq1, lk1, lq2, lk2, g_ref, o_ref, *, tq, t, off):
    qi = pl.program_id(1)
    t0 = qi * tq
    lam = _lam(lq1[...], lk1[...], lq2[...], lk2[...])
    row = lax.broadcasted_iota(jnp.int32, (tq, t), 0) + t0
    col = lax.broadcasted_iota(jnp.int32, (tq, t), 1)
    visible = (col // CHUNK) <= (row // CHUNK)
    start = pl.multiple_of(off - t0 - tq, LANES)
    for h in range(DIFF_HEADS):
        sl = slice(h * LANES, (h + 1) * LANES)
        q2, k2, v2 = q_ref[:, sl], k_ref[:, sl], v_ref[:, sl]
        bias = _toeplitz(f_ref[pl.ds(h, 1), pl.ds(start, tq + t)], tq, tq, t)
        s1 = jnp.where(visible, _qk(_half_select(q2, False), k2) + bias, NEG_INF)
        s2 = jnp.where(visible, _qk(_half_select(q2, True), k2) + bias, NEG_INF)
        e1, l1 = _softmax_parts(s1)
        e2, l2 = _softmax_parts(s2)
        w = e1 * (1.0 / l1) - e2 * (lam / l2)
        o = jnp.dot(w.astype(BF16), v2, preferred_element_type=F32)
        o_ref[:, sl] = _sub_ln(o, g_ref[...]).astype(BF16)


def _diff_prompt(q, k, v, ftab, off, lams, subln_g, tq):
    b, t, _ = q.shape
    kern = functools.partial(_diff_prompt_kernel, tq=tq, t=t, off=off)
    full = lambda a: pl.BlockSpec(a.shape, lambda i, j: (0,) * a.ndim)
    return pl.pallas_call(
        kern,
        grid=(b, t // tq),
        in_specs=[pl.BlockSpec((None, tq, DIFF_W), lambda i, j: (i, j, 0)),
                  pl.BlockSpec((None, t, DIFF_W), lambda i, j: (i, 0, 0)),
                  pl.BlockSpec((None, t, DIFF_W), lambda i, j: (i, 0, 0)),
                  full(ftab)] + [full(a) for a in lams] + [full(subln_g)],
        out_specs=pl.BlockSpec((None, tq, DIFF_W), lambda i, j: (i, j, 0)),
        out_shape=jax.ShapeDtypeStruct((b, t, DIFF_W), BF16),
        compiler_params=_cparams(("arbitrary", "arbitrary")),
        name="diff_prompt",
    )(q, k, v, ftab, *lams, subln_g)


def _diff_sample_kernel(q_ref, kn_ref, vn_ref, kc_ref, vc_ref, f_ref, lq1, lk1, lq2, lk2, g_ref, o_ref,
                        *, tq, p, pad):
    h = pl.program_id(1)
    lam = _lam(lq1[...], lk1[...], lq2[...], lk2[...])
    q2 = q_ref[...]
    kc = kc_ref[...].astype(BF16)
    vc = vc_ref[...].astype(BF16)
    kn, vn = kn_ref[...], vn_ref[...]
    bias_c = _toeplitz(f_ref[pl.ds(h, 1), 0:pad + p], tq, pad, p)
    bias_n = _toeplitz(f_ref[pl.ds(h, 1), p:p + 2 * pad], tq, pad, tq)
    parts = []
    for upper in (False, True):
        qh = _half_select(q2, upper)
        sc = _qk(qh, kc) + bias_c
        sn = _qk(qh, kn) + bias_n
        m = jnp.maximum(jnp.max(sc, axis=1, keepdims=True), jnp.max(sn, axis=1, keepdims=True))
        ec, en = jnp.exp(sc - m), jnp.exp(sn - m)
        l = jnp.sum(ec, axis=1, keepdims=True) + jnp.sum(en, axis=1, keepdims=True)
        parts.append((ec, en, l))
    (ec1, en1, l1), (ec2, en2, l2) = parts
    a1, a2 = 1.0 / l1, lam / l2
    wc = ec1 * a1 - ec2 * a2
    wn = en1 * a1 - en2 * a2
    o = (jnp.dot(wc.astype(BF16), vc, preferred_element_type=F32)
         + jnp.dot(wn.astype(BF16), vn, preferred_element_type=F32))
    o_ref[...] = _sub_ln(o, g_ref[...]).astype(BF16)


def _diff_sample(q, kn, vn, kc, vc, ftab, pad, lams, subln_g):
    b, tq, _ = q.shape
    p = kc.shape[1]
    kern = functools.partial(_diff_sample_kernel, tq=tq, p=p, pad=pad)
    new = pl.BlockSpec((None, tq, LANES), lambda i, g: (i, 0, g))
    old = pl.BlockSpec((None, p, LANES), lambda i, g: (i, 0, g))
    full = lambda a: pl.BlockSpec(a.shape, lambda i, g: (0,) * a.ndim)
    return pl.pallas_call(
        kern,
        grid=(b, DIFF_HEADS),
        in_specs=[new, new, new, old, old, full(ftab)] + [full(a) for a in lams] + [full(subln_g)],
        out_specs=new,
        out_shape=jax.ShapeDtypeStruct((b, tq, DIFF_W), BF16),
        compiler_params=_cparams(("arbitrary", "arbitrary")),
        name="diff_sample",
    )(q, kn, vn, kc, vc, ftab, *lams, subln_g)


def _layer_norm(x, g, b):
    mu = jnp.mean(x, axis=-1, keepdims=True)
    xc = x - mu
    var = jnp.mean(xc * xc, axis=-1, keepdims=True)
    return xc * lax.rsqrt(var + LN_EPS) * g + b


def _out_proj_kernel(x_ref, fox_ref, diff_ref, w_ref, g_ref, b_ref, o_ref):
    attn = (jnp.dot(fox_ref[...], w_ref[:FOX_W, :], preferred_element_type=F32)
            + jnp.dot(diff_ref[...], w_ref[FOX_W:, :], preferred_element_type=F32))
    o_ref[...] = _layer_norm(DEEP_ALPHA * x_ref[...] + attn, g_ref[...], b_ref[...])


def _out_proj(x2d, fox, diff, w_out, g, b, tm):
    n = x2d.shape[0]
    row = lambda w: pl.BlockSpec((tm, w), lambda i: (i, 0))
    full = lambda a: pl.BlockSpec(a.shape, lambda i: (0,) * a.ndim)
    return pl.pallas_call(
        _out_proj_kernel,
        grid=(n // tm,),
        in_specs=[row(D_MODEL), row(FOX_W), row(DIFF_W), full(w_out), full(g), full(b)],
        out_specs=row(D_MODEL),
        out_shape=jax.ShapeDtypeStruct((n, D_MODEL), F32),
        compiler_params=_cparams(("arbitrary",)),
        name="out_proj",
    )(x2d, fox, diff, w_out, g, b)


def _batcher_pairs(n):
    pairs = []
    p = 1
    while p < n:
        k = p
        while k >= 1:
            for j in range(k % p, n - k, 2 * k):
                for i in range(min(k, n - j - k)):
                    if (i + j) // (2 * p) == (i + j + k) // (2 * p):
                        pairs.append((i + j, i + j + k))
            k //= 2
        p *= 2
    return pairs


def _sort_desc(vals):
    vals = list(vals)
    for i, j in _batcher_pairs(len(vals)):
        a, b = vals[i], vals[j]
        if b is None:
            continue
        if a is None:
            vals[i], vals[j] = b, None
        else:
            vals[i], vals[j] = jnp.maximum(a, b), jnp.minimum(a, b)
    return vals


def _merge_top(a, b):
    n = len(a)
    v = [jnp.maximum(a[i], b[n - 1 - i]) for i in range(n)]
    d = n // 2
    while d >= 1:
        for i in range(n):
            if (i & d) == 0:
                v[i], v[i + d] = jnp.maximum(v[i], v[i + d]), jnp.minimum(v[i], v[i + d])
        d //= 2
    return v


def _top16_sublane_slab(s):
    k = PEER_TOPK
    v = _sort_desc([s[SUBLANES * i:SUBLANES * (i + 1), :] for i in range(PEER_KEYS // SUBLANES)])
    shift = SUBLANES // 2
    while shift >= 1:
        v = _merge_top(v, [pltpu.roll(x, shift, 0) for x in v])
        shift //= 2
    return v[:k]


def _peer_kernel(x_ref, wqt_ref, keys_ref, u_ref, vt_ref, g_ref, b_ref, o_ref,
                 xt_scr, s_scr, e_scr, tau_scr, yt_scr, *, tt, ne):
    eb = pl.program_id(1)
    nh = PEER_HEADS

    @pl.when(eb == 0)
    def _prepare():
        xt = jnp.transpose(x_ref[...]).astype(BF16)
        xt_scr[...] = xt
        qt = jnp.dot(wqt_ref[...], xt, preferred_element_type=F32).astype(BF16)
        sub = lax.broadcasted_iota(jnp.int32, (SUBLANES, tt), 0)
        tops = [[None] * PEER_TOPK, [None] * PEER_TOPK]
        for h in range(nh):
            for p in range(2):
                r = (2 * h + p) * PEER_KEYS
                s = jnp.dot(keys_ref[r:r + PEER_KEYS, :], qt[r:r + PEER_KEYS, :], preferred_element_type=F32)
                s_scr[p, h] = s
                top = _top16_sublane_slab(s)
                for kk in range(PEER_TOPK):
                    prev = tops[p][kk]
                    tops[p][kk] = top[kk] if prev is None else jnp.where(sub == h, top[kk], prev)
        a, b = tops
        cands = [a[i] + b[j] for i in range(PEER_TOPK) for j in range(PEER_TOPK) if (i + 1) * (j + 1) <= PEER_TOPK]
        cands = cands + [None] * (64 - len(cands))
        top = _sort_desc(cands)[:PEER_TOPK]
        z = jnp.zeros_like(top[0])
        for tv in top:
            z = z + jnp.exp(tv - top[0])
        inv_z = 1.0 / z
        tau_scr[...] = top[PEER_TOPK - 1]
        for h in range(nh):
            e_scr[0, h] = jnp.exp(s_scr[0, h] - a[0][h:h + 1, :]) * inv_z[h:h + 1, :]
            e_scr[1, h] = jnp.exp(s_scr[1, h] - b[0][h:h + 1, :])
        yt_scr[...] = jnp.zeros_like(yt_scr)

    hid = jnp.dot(u_ref[...], xt_scr[...], preferred_element_type=F32)
    acts = []
    for il in range(ne // PEER_KEYS):
        i = eb * (ne // PEER_KEYS) + il
        gate = jnp.zeros((PEER_KEYS, tt), F32)
        for h in range(nh):
            s0 = s_scr[0, h, pl.ds(i, 1), :]
            e0 = e_scr[0, h, pl.ds(i, 1), :]
            sel = (s_scr[1, h] + s0) >= tau_scr[h:h + 1, :]
            gate = gate + jnp.where(sel, e_scr[1, h] * e0, 0.0)
        hh = hid[il * PEER_KEYS:(il + 1) * PEER_KEYS, :]
        gelu = 0.5 * hh * (1.0 + lax.erf(hh * (2.0 ** -0.5)))
        acts.append((gate * gelu).astype(BF16))
    act = jnp.concatenate(acts, axis=0)
    yt_scr[...] += jnp.dot(vt_ref[...], act, preferred_element_type=F32)

    @pl.when(eb == pl.num_programs(1) - 1)
    def _finish():
        y = jnp.transpose(yt_scr[...])
        o_ref[...] = _layer_norm(DEEP_ALPHA * x_ref[...] + y, g_ref[...], b_ref[...])


def _peer(x2d, wqt, keys2d, u_b, vt_b, g, b, tt, ne):
    n = x2d.shape[0]
    n_exp = u_b.shape[0]
    kern = functools.partial(_peer_kernel, tt=tt, ne=ne)
    full = lambda a: pl.BlockSpec(a.shape, lambda i, e: (0,) * a.ndim)
    return pl.pallas_call(
        kern,
        grid=(n // tt, n_exp // ne),
        in_specs=[pl.BlockSpec((tt, D_MODEL), lambda i, e: (i, 0)),
                  full(wqt), full(keys2d),
                  pl.BlockSpec((ne, D_MODEL), lambda i, e: (e, 0)),
                  pl.BlockSpec((D_MODEL, ne), lambda i, e: (0, e)),
                  full(g), full(b)],
        out_specs=pl.BlockSpec((tt, D_MODEL), lambda i, e: (i, 0)),
        out_shape=jax.ShapeDtypeStruct((n, D_MODEL), F32),
        scratch_shapes=[pltpu.VMEM((D_MODEL, tt), BF16),
                        pltpu.VMEM((2, PEER_HEADS, PEER_KEYS, tt), F32),
                        pltpu.VMEM((2, PEER_HEADS, PEER_KEYS, tt), F32),
                        pltpu.VMEM((PEER_HEADS, tt), F32),
                        pltpu.VMEM((D_MODEL, tt), F32)],
        compiler_params=_cparams(("arbitrary", "arbitrary")),
        name="peer",
    )(x2d, wqt, keys2d, u_b, vt_b, g, b)


TM_PROJ = 512
TQ_ATTN = 256
TT_PEER = 512
NE_PEER = 1024


def _prep_weights(w_in, w_out, peer_wq, peer_subkeys, peer_u, peer_v):
    cuts = np.cumsum([0, FOX_W, FOX_W, FOX_W, FOX_HEADS, DIFF_W, DIFF_W, DIFF_W])
    seg = lambda i: w_in[:, cuts[i]:cuts[i + 1]]
    scale = FOX_DH ** -0.5
    w_main = jnp.concatenate([seg(0) * scale, seg(1), seg(2), seg(4) * (DIFF_DH ** -0.5), seg(5), seg(6)], axis=1)
    w_g = jnp.pad(seg(3), ((0, 0), (0, LANES - FOX_HEADS)))
    w_gt = jnp.pad(seg(3).T, ((0, 2 * SUBLANES - FOX_HEADS), (0, 0)))
    keys2d = peer_subkeys.reshape(PEER_HEADS * 2 * PEER_KEYS, PEER_QDIM // 2)
    return (w_main.astype(BF16), w_g.astype(BF16), w_gt.astype(BF16), w_out.astype(BF16),
            peer_wq.T.astype(BF16), keys2d.astype(BF16), peer_u.astype(BF16), peer_v.T.astype(BF16))


def kernel(x_prompt, x_sample, cache_fox_k, cache_fox_v, cache_fox_logf, cache_diff_k, cache_diff_v, w_in, b_f, lam_q1, lam_k1, lam_q2, lam_k2, subln_g, w_out, rel_bias, ln1_g, ln1_b, ln2_g, ln2_b, peer_wq, peer_subkeys, peer_u, peer_v):
    l = 0
    bp, tp, _ = x_prompt.shape
    bs, ts, _ = x_sample.shape
    past = cache_fox_k.shape[2]
    w_main, w_g, w_gt, w_o, wqt, keys2d, u_b, vt_b = _prep_weights(
        w_in[l], w_out[l], peer_wq[l], peer_subkeys[l], peer_u[l], peer_v[l])
    lams = (lam_q1[l][None], lam_k1[l][None], lam_q2[l][None], lam_k2[l][None])
    sg = subln_g[l][None]
    ln1 = (ln1_g[l][None], ln1_b[l][None])
    ln2 = (ln2_g[l][None], ln2_b[l][None])

    def trunk(x, cache):
        b, t, _ = x.shape
        n = b * t
        x2d = x.reshape(n, D_MODEL)
        fk, fv, dk, dv, logf, logft, fqb, fkb, fvb, dqb, dkb, dvb = _in_proj(x2d, w_main, w_g, w_gt, b_f[l], min(TM_PROJ, n))
        r3 = lambda a: a.reshape(b, t, a.shape[-1])
        lft = jnp.transpose(logft.reshape(FOX_HEADS, b, t), (1, 0, 2))
        if cache is None:
            off = t
            ftab = _bias_table(rel_bias, off, 2 * t)
            fox = _fox_prompt(r3(fqb), r3(fkb), r3(fvb), lft, TQ_ATTN)
            diff = _diff_prompt(r3(dqb), r3(dkb), r3(dvb), ftab, off, lams, sg, TQ_ATTN)
        else:
            c_fk, c_fv, c_lf, c_dk, c_dv = cache
            pad = LANES
            ftab = _bias_table(rel_bias, past + pad, past + 2 * pad)
            fox = _fox_sample(r3(fqb), r3(fkb), r3(fvb), c_fk.reshape(b, past, FOX_W), c_fv.reshape(b, past, FOX_W),
                              jnp.swapaxes(c_lf, 1, 2), jnp.pad(lft, ((0, 0), (0, 0), (0, LANES - t))))
            diff = _diff_sample(r3(dqb), r3(dkb), r3(dvb), c_dk.reshape(b, past, DIFF_W),
                                c_dv.reshape(b, past, DIFF_W), ftab, pad, lams, sg)
        x1 = _out_proj(x2d, fox.reshape(n, FOX_W), diff.reshape(n, DIFF_W), w_o, *ln1, min(TM_PROJ, n))
        y = _peer(x1, wqt, keys2d, u_b, vt_b, *ln2, min(TT_PEER, n), NE_PEER)
        new = (fk.reshape(1, b, t, FOX_HEADS, FOX_DH), fv.reshape(1, b, t, FOX_HEADS, FOX_DH),
               logf.reshape(1, b, t, FOX_HEADS), dk.reshape(1, b, t, DIFF_HEADS, 2 * DIFF_DH),
               dv.reshape(1, b, t, DIFF_HEADS, 2 * DIFF_DH))
        return y.reshape(b, t, D_MODEL), new

    y_p, new_p = trunk(x_prompt, None)
    y_s, new_s = trunk(x_sample, (cache_fox_k[l], cache_fox_v[l], cache_fox_logf[l], cache_diff_k[l], cache_diff_v[l]))
    return (y_p, y_s) + new_p + new_s
```
